```python
import jax, jax.numpy as jnp
from jax import lax
import numpy as np

D_MODEL = 1024
BATCH = 16
SEQ = 256
DEPTH = 2
DEC_BATCH = 4
DEC_SEQ = 1024
PAST_LEN = 256

GRID_W = 64
N_EVEN = (DEPTH + 1) // 2
N_ODD = DEPTH // 2
NORM_EPS = 1e-6
ROPE_BASE = 10000.0
HA = 4
DKA = 128
DVA = 128
GLA_CHUNK = 16
HB = 4
DKB = 128
DVB = 128
MLSTM_CHUNK = 64
HQ_C = 8
HKV_C = 2
G_C = HQ_C // HKV_C
HD_C = 64
WINDOW = 128
BAND_BLOCK = 128
H_D = 8
D_CQ = 384
D_C = 256
NOPE = 64
ROPE = 32
VD = 64
Q_BLOCK = 128
D_FF = -(-8 * D_MODEL // (3 * 256)) * 256
MIX_EVEN = HA * DVA + HB * DVB
MIX_ODD = HQ_C * HD_C + H_D * VD
EVEN_SPLITS = (HA * DKA, HA * DKA, HA * DKA, HA * DVA, HA * DVA, HB * DKB, HB * DKB, HB * DVB, HB * DVB, 4 * HB)
ODD_SPLITS = (HQ_C * HD_C, HKV_C * HD_C, HKV_C * HD_C, D_CQ, D_C, ROPE)
P_EVEN = sum(EVEN_SPLITS)
P_ODD = sum(ODD_SPLITS)
F32 = jnp.float32

kernel_name = 'hybrid_diffusion_prefix_trunk_step'


def rms_norm(x, g):
    x32 = x.astype(F32)
    y = x32 * lax.rsqrt(jnp.mean(x32 * x32, axis=-1, keepdims=True) + NORM_EPS)
    return y.astype(x.dtype) * g


def split_cols(p, sizes):
    return jnp.split(p, np.cumsum(sizes)[:-1].tolist(), axis=-1)


def axial_rope(x):
    T, d = x.shape[1], x.shape[-1]
    n_rows = T // GRID_W
    rows = jnp.repeat(jnp.arange(n_rows, dtype=F32), GRID_W)
    cols = jnp.tile(jnp.arange(GRID_W, dtype=F32), n_rows)
    d_ax = d // 2
    inv_freq = ROPE_BASE ** (-jnp.arange(0, d_ax, 2, dtype=F32) / d_ax)
    bshape = (1, T) + (1,) * (x.ndim - 3) + (d_ax // 2,)

    def rotate(xh, pos):
        ang = pos[:, None] * inv_freq[None, :]
        cos, sin = jnp.cos(ang).reshape(bshape), jnp.sin(ang).reshape(bshape)
        x1, x2 = jnp.split(xh, 2, axis=-1)
        return jnp.concatenate([x1 * cos - x2 * sin, x1 * sin + x2 * cos], axis=-1)

    x32 = x.astype(F32)
    out = jnp.concatenate([rotate(x32[..., :d_ax], rows), rotate(x32[..., d_ax:], cols)], axis=-1)
    return out.astype(x.dtype)


def gla_chunked(q, k, v, logf, s0):
    B, T, H, K = q.shape
    L = GLA_CHUNK
    N = T // L
    q, k, v, logf = [a.astype(F32).reshape((B, N, L, H, a.shape[-1])) for a in (q, k, v, logf)]
    b = jnp.cumsum(logf, axis=2)
    causal = jnp.tril(jnp.ones((L, L), bool))[None, None, :, :, None, None]
    diff = b[:, :, :, None] - b[:, :, None, :]
    decay = jnp.exp(jnp.where(causal, diff, -jnp.inf))
    attn = jnp.einsum('bntshk,bnthk,bnshk->bnhts', decay, q, k)
    o_intra = jnp.einsum('bnhts,bnshv->bnthv', attn, v)
    b_last = b[:, :, -1]
    kv = jnp.einsum('bnshk,bnshv->bnhkv', k * jnp.exp(b_last[:, :, None] - b), v)

    def step(S, inp):
        g, kv_c = inp
        return g[..., None] * S + kv_c, S

    s_fin, s_prev = lax.scan(step, s0.astype(F32), (jnp.moveaxis(jnp.exp(b_last), 1, 0), jnp.moveaxis(kv, 1, 0)))
    o_inter = jnp.einsum('bnthk,bnhkv->bnthv', q * jnp.exp(b), jnp.moveaxis(s_prev, 0, 1))
    return (o_intra + o_inter).reshape(B, T, H, -1), s_fin


def mlstm_chunked(q, k, v, ig, logf, c0, n0, m0):
    B, T, H, K = q.shape
    L = MLSTM_CHUNK
    N = T // L
    chunk = lambda a: jnp.moveaxis(a.astype(F32).reshape((B, N, L) + a.shape[2:]), 1, 0)
    causal = jnp.tril(jnp.ones((L, L), bool))[None, :, :, None]

    def step(carry, inp):
        C, n, m = carry
        qc, kc, vc, ic, fc = inp
        b = jnp.cumsum(fc, axis=1)
        w_state = b + m[:, None, :]
        w_in = jnp.where(causal, b[:, :, None, :] - b[:, None, :, :] + ic[:, None, :, :], -jnp.inf)
        m_t = jnp.maximum(w_state, jnp.max(w_in, axis=2))
        qk = jnp.einsum('bthk,bshk->btsh', qc, kc) * jnp.exp(w_in - m_t[:, :, None, :])
        s_state = jnp.exp(w_state - m_t)
        num = jnp.einsum('btsh,bshv->bthv', qk, vc) + s_state[..., None] * jnp.einsum('bthk,bhkv->bthv', qc, C)
        den = jnp.sum(qk, axis=2) + s_state * jnp.einsum('bthk,bhk->bth', qc, n)
        h = num / jnp.maximum(jnp.abs(den), jnp.exp(-m_t))[..., None]
        b_last = b[:, -1]
        w_end = b_last[:, None, :] - b + ic
        m_new = jnp.maximum(b_last + m, jnp.max(w_end, axis=1))
        a_state = jnp.exp(b_last + m - m_new)
        a_in = jnp.exp(w_end - m_new[:, None, :])
        c_new = a_state[..., None, None] * C + jnp.einsum('bsh,bshk,bshv->bhkv', a_in, kc, vc)
        n_new = a_state[..., None] * n + jnp.einsum('bsh,bshk->bhk', a_in, kc)
        return (c_new, n_new, m_new), h

    (C, n, m), h = lax.scan(step, (c0.astype(F32), n0.astype(F32), m0.astype(F32)),
                            (chunk(q), chunk(k), chunk(v), chunk(ig), chunk(logf)))
    return jnp.moveaxis(h, 0, 1).reshape(B, T, H, -1), C, n, m


def attend_dense(q, k, v, scale, sink=None):
    B, Tq, Hk, G, dq = q.shape
    nb = Tq // Q_BLOCK
    qb = jnp.moveaxis(q.reshape(B, nb, Q_BLOCK, Hk, G, dq), 1, 0)

    def one_block(qblk):
        s = jnp.einsum('bqhgd,bkhd->bhgqk', qblk, k).astype(F32) * scale
        if sink is not None:
            s_sink = jnp.broadcast_to(sink.astype(F32)[None, :, :, None, None], s.shape[:-1] + (1,))
            p = jax.nn.softmax(jnp.concatenate([s, s_sink], axis=-1), axis=-1)[..., :-1]
        else:
            p = jax.nn.softmax(s, axis=-1)
        return jnp.einsum('bhgqk,bkhv->bqhgv', p.astype(v.dtype), v)

    o = lax.map(one_block, qb)
    return jnp.moveaxis(o, 0, 1).reshape(B, Tq, Hk, G, -1)


def attend_window_with_context(q, k, v, k_ctx, v_ctx, sink, scale):
    B, T, Hk, G, d = q.shape
    P = k_ctx.shape[1]
    nb = T // BAND_BLOCK
    qb = q.reshape(B, nb, BAND_BLOCK, Hk, G, d)

    def band(a):
        ap = jnp.pad(a, ((0, 0), (BAND_BLOCK, BAND_BLOCK), (0, 0), (0, 0)))
        ap = ap.reshape(B, nb + 2, BAND_BLOCK, Hk, a.shape[-1])
        return jnp.concatenate([ap[:, :nb], ap[:, 1:nb + 1], ap[:, 2:]], axis=2)

    kb, vb = band(k), band(v)
    qi = jnp.arange(BAND_BLOCK)[:, None]
    kj = jnp.arange(3 * BAND_BLOCK)[None, :]
    kpos = jnp.arange(nb)[:, None, None] * BAND_BLOCK - BAND_BLOCK + kj[None]
    valid = (jnp.abs(kj - BAND_BLOCK - qi)[None] <= WINDOW) & (kpos >= 0) & (kpos < T)
    s_band = jnp.einsum('bnqhgd,bnkhd->bnhgqk', qb, kb).astype(F32) * scale
    s_band = jnp.where(valid[None, :, None, None], s_band, -jnp.inf)
    s_ctx = jnp.einsum('bnqhgd,bphd->bnhgqp', qb, k_ctx).astype(F32) * scale
    s_sink = jnp.broadcast_to(sink.astype(F32).reshape(1, 1, Hk, G, 1, 1), s_band.shape[:-1] + (1,))
    p = jax.nn.softmax(jnp.concatenate([s_band, s_ctx, s_sink], axis=-1), axis=-1)
    p_band = p[..., :3 * BAND_BLOCK].astype(v.dtype)
    p_ctx = p[..., 3 * BAND_BLOCK:3 * BAND_BLOCK + P].astype(v.dtype)
    o = jnp.einsum('bnhgqk,bnkhd->bnqhgd', p_band, vb) + jnp.einsum('bnhgqp,bphd->bnqhgd', p_ctx, v_ctx)
    return o.reshape(B, T, Hk, G, d)


def adaln(cond, w, b):
    m = jax.nn.silu(cond) @ w + b
    if m.ndim == 2:
        m = m[:, None, :]
    return jnp.split(m, 6, axis=-1)


def swiglu(h, w1, w3, w2):
    return (jax.nn.silu(h @ w1) * (h @ w3)) @ w2


def even_mixer(h, w_in, lb, hgrn_g, gate_bias, mlstm_g, w_out, s_hgrn, s_c, s_n, s_m):
    B, T, _ = h.shape
    qa, fa_f, fa_b, ia, ga, qb, kb, vb, ob, gates = split_cols(h @ w_in, EVEN_SPLITS)
    heads = lambda a, n: a.astype(F32).reshape(B, T, n, -1)
    qa = heads(jax.nn.silu(qa), HA)
    ia = heads(ia, HA)

    def hgrn_dir(f_raw, lbd, s0, flip):
        f = heads(lbd + (1.0 - lbd) * jax.nn.sigmoid(f_raw.astype(F32)), HA)
        args = (qa, 1.0 - f, ia, jnp.log(f))
        if flip:
            args = tuple(jnp.flip(a, axis=1) for a in args)
        o, s = gla_chunked(*args, s0)
        return (jnp.flip(o, axis=1) if flip else o), s

    oa_f, sa_f = hgrn_dir(fa_f, lb[0], s_hgrn[:, 0], False)
    oa_b, sa_b = hgrn_dir(fa_b, lb[1], s_hgrn[:, 1], True)
    oa = rms_norm(oa_f + oa_b, hgrn_g) * jax.nn.silu(heads(ga, HA))
    qb = heads(qb, HB) * (DKB ** -0.5)
    kb = heads(kb, HB)
    vb = heads(vb, HB)
    g = gates.astype(F32).reshape(B, T, 4, HB) + gate_bias.astype(F32)

    def mlstm_dir(ig, fg, c0, n0, m0, flip):
        args = (qb, kb, vb, ig, jax.nn.log_sigmoid(fg))
        if flip:
            args = tuple(jnp.flip(a, axis=1) for a in args)
        o, C, n, m = mlstm_chunked(*args, c0, n0, m0)
        return (jnp.flip(o, axis=1) if flip else o), C, n, m

    hb_f, c_f, n_f, m_f = mlstm_dir(g[:, :, 0], g[:, :, 1], s_c[:, 0], s_n[:, 0], s_m[:, 0], False)
    hb_b, c_b, n_b, m_b = mlstm_dir(g[:, :, 2], g[:, :, 3], s_c[:, 1], s_n[:, 1], s_m[:, 1], True)
    hb = rms_norm(hb_f + hb_b, mlstm_g) * jax.nn.sigmoid(heads(ob, HB))
    mix = jnp.concatenate([oa.reshape(B, T, -1), hb.reshape(B, T, -1)], axis=-1).astype(h.dtype)
    new = (jnp.stack([sa_f, sa_b], 1), jnp.stack([c_f, c_b], 1), jnp.stack([n_f, n_b], 1), jnp.stack([m_f, m_b], 1))
    return mix @ w_out, new


def odd_project(h, w_in, q_norm_g, w_uq, kv_norm_g):
    B, T, _ = h.shape
    qc, kc, vc, cq, ckv, kr = split_cols(h @ w_in, ODD_SPLITS)
    qc = qc.reshape(B, T, HKV_C, G_C, HD_C)
    kc = kc.reshape(B, T, HKV_C, HD_C)
    vc = vc.reshape(B, T, HKV_C, HD_C)
    qm = (rms_norm(cq, q_norm_g) @ w_uq).reshape(B, T, H_D, NOPE + ROPE)
    return qc, kc, vc, qm, rms_norm(ckv, kv_norm_g), kr


def mla_kv(ckv, kr, w_uk, w_uv):
    B, T, _ = ckv.shape
    k_nope = (ckv @ w_uk).reshape(B, T, H_D, NOPE)
    k = jnp.concatenate([k_nope, jnp.broadcast_to(kr[:, :, None, :], (B, T, H_D, ROPE)).astype(k_nope.dtype)], axis=-1)
    return k, (ckv @ w_uv).reshape(B, T, H_D, VD)


def odd_mixer_context(h, w_in, sink, q_norm_g, w_uq, kv_norm_g, w_uk, w_uv, w_out):
    B, T, _ = h.shape
    qc, kc, vc, qm, ckv, kr = odd_project(h, w_in, q_norm_g, w_uq, kv_norm_g)
    oc = attend_dense(qc, kc, vc, HD_C ** -0.5, sink.reshape(HKV_C, G_C))
    km, vm = mla_kv(ckv, kr, w_uk, w_uv)
    od = attend_dense(qm[:, :, :, None, :], km, vm, (NOPE + ROPE) ** -0.5)
    mix = jnp.concatenate([oc.reshape(B, T, -1), od.reshape(B, T, -1)], axis=-1)
    return mix @ w_out, kc, vc, ckv, kr


def odd_mixer_latent(h, w_in, sink, q_norm_g, w_uq, kv_norm_g, w_uk, w_uv, w_out, k_ctx, v_ctx, ckv_ctx, kr_ctx):
    B, T, _ = h.shape
    qc, kc, vc, qm, ckv, kr = odd_project(h, w_in, q_norm_g, w_uq, kv_norm_g)
    oc = attend_window_with_context(axial_rope(qc), axial_rope(kc), vc, k_ctx, v_ctx,
                                    sink.reshape(HKV_C, G_C), HD_C ** -0.5)
    qm = jnp.concatenate([qm[..., :NOPE], axial_rope(qm[..., NOPE:])], axis=-1)
    km, vm = mla_kv(jnp.concatenate([ckv, ckv_ctx.astype(ckv.dtype)], axis=1),
                    jnp.concatenate([axial_rope(kr), kr_ctx.astype(kr.dtype)], axis=1), w_uk, w_uv)
    od = attend_dense(qm[:, :, :, None, :], km, vm, (NOPE + ROPE) ** -0.5)
    mix = jnp.concatenate([oc.reshape(B, T, -1), od.reshape(B, T, -1)], axis=-1)
    return mix @ w_out


def setup_inputs(seed: int = 0) -> dict:
    key = jax.random.key(seed)
    ks = iter(jax.random.split(key, 48))
    nrm = lambda shape, s: s * jax.random.normal(next(ks), shape, F32)
    gain = lambda shape: 1.0 + nrm(shape, 0.02)
    fb = jnp.linspace(3.0, 6.0, HB, dtype=F32)
    gate_sel = jnp.array([0.0, 1.0, 0.0, 1.0], F32)
    return {
        'x_prompt': nrm((BATCH, SEQ, D_MODEL), 1.0),
        'x_sample': nrm((DEC_BATCH, DEC_SEQ, D_MODEL), 1.0),
        'state_hgrn': nrm((DEC_BATCH, N_EVEN, 2, HA, DKA, DVA), 0.5),
        'state_mlstm_c': nrm((DEC_BATCH, N_EVEN, 2, HB, DKB, DVB), 0.5),
        'state_mlstm_n': nrm((DEC_BATCH, N_EVEN, 2, HB, DKB), 0.5),
        'state_mlstm_m': nrm((DEC_BATCH, N_EVEN, 2, HB), 1.0),
        'cache_swa_k': nrm((DEC_BATCH, N_ODD, PAST_LEN, HKV_C, HD_C), 1.0),
        'cache_swa_v': nrm((DEC_BATCH, N_ODD, PAST_LEN, HKV_C, HD_C), 1.0),
        'cache_mla_ckv': nrm((DEC_BATCH, N_ODD, PAST_LEN, D_C), 1.0),
        'cache_mla_krope': nrm((DEC_BATCH, N_ODD, PAST_LEN, ROPE), 1.0),
        'c': nrm((DEC_BATCH, D_MODEL), 1.0),
        'c_ctx': nrm((D_MODEL,), 1.0),
        'w_in_even': nrm((N_EVEN, D_MODEL, P_EVEN), D_MODEL ** -0.5),
        'hgrn_lower_bounds': nrm((N_EVEN + 1, 2, HA * DKA), 0.5),
        'hgrn_norm_g': gain((N_EVEN, DVA)),
        'mlstm_gate_bias': nrm((N_EVEN, 4, HB), 0.1) + gate_sel[None, :, None] * fb[None, None, :],
        'mlstm_norm_g': gain((N_EVEN, DVB)),
        'w_out_even': nrm((N_EVEN, MIX_EVEN, D_MODEL), MIX_EVEN ** -0.5),
        'w_in_odd': nrm((N_ODD, D_MODEL, P_ODD), D_MODEL ** -0.5),
        'swa_sink': nrm((N_ODD, HQ_C), 1.0),
        'mla_q_norm_g': gain((N_ODD, D_CQ)),
        'mla_w_uq': nrm((N_ODD, D_CQ, H_D * (NOPE + ROPE)), D_CQ ** -0.5),
        'mla_kv_norm_g': gain((N_ODD, D_C)),
        'mla_w_uk': nrm((N_ODD, D_C, H_D * NOPE), D_C ** -0.5),
        'mla_w_uv': nrm((N_ODD, D_C, H_D * VD), D_C ** -0.5),
        'w_out_odd': nrm((N_ODD, MIX_ODD, D_MODEL), MIX_ODD ** -0.5),
        'ada_w': nrm((DEPTH, D_MODEL, 6 * D_MODEL), D_MODEL ** -0.5),
        'ada_b': nrm((DEPTH, 6 * D_MODEL), 0.02),
        'norm1_g': gain((DEPTH, D_MODEL)),
        'norm2_g': gain((DEPTH, D_MODEL)),
        'ffn_w1': nrm((DEPTH, D_MODEL, D_FF), D_MODEL ** -0.5),
        'ffn_w3': nrm((DEPTH, D_MODEL, D_FF), D_MODEL ** -0.5),
        'ffn_w2': nrm((DEPTH, D_FF, D_MODEL), D_FF ** -0.5),
        'final_norm_g': gain((D_MODEL,)),
    }


def reference(x_prompt, x_sample, state_hgrn, state_mlstm_c, state_mlstm_n, state_mlstm_m,
              cache_swa_k, cache_swa_v, cache_mla_ckv, cache_mla_krope, c, c_ctx,
              w_in_even, hgrn_lower_bounds, hgrn_norm_g, mlstm_gate_bias, mlstm_norm_g, w_out_even,
              w_in_odd, swa_sink, mla_q_norm_g, mla_w_uq, mla_kv_norm_g, mla_w_uk, mla_w_uv, w_out_odd,
              ada_w, ada_b, norm1_g, norm2_g, ffn_w1, ffn_w3, ffn_w2, final_norm_g):
    lb_all = jnp.cumsum(jax.nn.softmax(hgrn_lower_bounds.astype(F32), axis=0), axis=0)
    even_p = lambda j: (w_in_even[j], lb_all[j], hgrn_norm_g[j], mlstm_gate_bias[j], mlstm_norm_g[j], w_out_even[j])
    odd_p = lambda j: (w_in_odd[j], swa_sink[j], mla_q_norm_g[j], mla_w_uq[j], mla_kv_norm_g[j],
                       mla_w_uk[j], mla_w_uv[j], w_out_odd[j])

    x = x_prompt
    bp = x.shape[0]
    hg, mc, mn, mm, sk, sv, ck, cr = ([] for _ in range(8))
    for l in range(DEPTH):
        j = l // 2
        sh1, sc1, g1, sh2, sc2, g2 = adaln(c_ctx, ada_w[l], ada_b[l])
        h = rms_norm(x, norm1_g[l]) * (1.0 + sc1) + sh1
        if l % 2 == 0:
            out, (s_h, s_c, s_n, s_m) = even_mixer(
                h, *even_p(j),
                jnp.zeros((bp, 2, HA, DKA, DVA), F32), jnp.zeros((bp, 2, HB, DKB, DVB), F32),
                jnp.zeros((bp, 2, HB, DKB), F32), jnp.zeros((bp, 2, HB), F32))
            hg.append(s_h)
            mc.append(s_c)
            mn.append(s_n)
            mm.append(s_m)
        else:
            out, kc, vc, ckv, kr = odd_mixer_context(h, *odd_p(j))
            sk.append(kc)
            sv.append(vc)
            ck.append(ckv)
            cr.append(kr)
        x = x + g1 * out
        h = rms_norm(x, norm2_g[l]) * (1.0 + sc2) + sh2
        x = x + g2 * swiglu(h, ffn_w1[l], ffn_w3[l], ffn_w2[l])
    y_prompt = rms_norm(x, final_norm_g)

    x = x_sample
    for l in range(DEPTH):
        j = l // 2
        sh1, sc1, g1, sh2, sc2, g2 = adaln(c, ada_w[l], ada_b[l])
        h = rms_norm(x, norm1_g[l]) * (1.0 + sc1) + sh1
        if l % 2 == 0:
            out, _ = even_mixer(h, *even_p(j), state_hgrn[:, j], state_mlstm_c[:, j],
                                state_mlstm_n[:, j], state_mlstm_m[:, j])
        else:
            out = odd_mixer_latent(h, *odd_p(j), cache_swa_k[:, j], cache_swa_v[:, j],
                                   cache_mla_ckv[:, j], cache_mla_krope[:, j])
        x = x + g1 * out
        h = rms_norm(x, norm2_g[l]) * (1.0 + sc2) + sh2
        x = x + g2 * swiglu(h, ffn_w1[l], ffn_w3[l], ffn_w2[l])
    y_sample = rms_norm(x, final_norm_g)

    new_state_hgrn = jnp.stack(hg, axis=1)
    new_state_mlstm_c = jnp.stack(mc, axis=1)
    new_state_mlstm_n = jnp.stack(mn, axis=1)
    new_state_mlstm_m = jnp.stack(mm, axis=1)
    new_cache_swa_k = jnp.stack(sk, axis=1)
    new_cache_swa_v = jnp.stack(sv, axis=1)
    new_cache_mla_ckv = jnp.stack(ck, axis=1)
    new_cache_mla_krope = jnp.stack(cr, axis=1)
    return (y_prompt, y_sample, new_state_hgrn, new_state_mlstm_c, new_state_mlstm_n, new_state_mlstm_m,
            new_cache_swa_k, new_cache_swa_v, new_cache_mla_ckv, new_cache_mla_krope)
```

```python
import functools

import numpy as np
import jax
import jax.numpy as jnp
from jax import lax
from jax.experimental import pallas as pl
from jax.experimental.pallas import tpu as pltpu

F32 = jnp.float32
BF = jnp.bfloat16

D_MODEL = 1024
DEPTH = 2
GRID_W = 64
NORM_EPS = 1e-6
ROPE_BASE = 10000.0
HA, DKA, DVA = 4, 128, 128
HB, DKB, DVB = 4, 128, 128
HQ_C, HKV_C, HD_C = 8, 2, 64
G_C = HQ_C // HKV_C
WINDOW = 128
H_D, D_CQ, D_C, NOPE, ROPE, VD = 8, 384, 256, 64, 32, 64
D_FF = -(-8 * D_MODEL // (3 * 256)) * 256
P_EVEN = 5 * HA * DKA + 4 * HB * DKB + 4 * HB
P_ODD = HQ_C * HD_C + 2 * HKV_C * HD_C + D_CQ + D_C + ROPE
LANE = 128
P_EVEN_PAD = -(-P_EVEN // LANE) * LANE
P_ODD_PAD = -(-P_ODD // LANE) * LANE
CHUNK = 64
VMEM_LIMIT = 56 * 1024 * 1024
NEG = -1e30


def _dot(a, b):
    return jnp.dot(a, b, preferred_element_type=F32)


def _dot_nt(a, b):
    return lax.dot_general(a, b, (((1,), (1,)), ((), ())), preferred_element_type=F32)


def _dot_tn(a, b):
    return lax.dot_general(a, b, (((0,), (0,)), ((), ())), preferred_element_type=F32)


def _rms(x):
    return x * lax.rsqrt(jnp.mean(x * x, axis=-1, keepdims=True) + NORM_EPS)


def _sigmoid(x):
    return 1.0 / (1.0 + jnp.exp(-x))


def _silu(x):
    return x * _sigmoid(x)


def _params(n_grid):
    return pltpu.CompilerParams(dimension_semantics=("arbitrary",) * n_grid, vmem_limit_bytes=VMEM_LIMIT)


def _resident(shape):
    nd = len(shape)
    return pl.BlockSpec(shape, lambda *_: (0,) * nd, pipeline_mode=pl.Buffered(1))


def _ada_kernel(c_ref, w_ref, b_ref, o_ref):
    s = _silu(c_ref[...]).astype(BF)
    o_ref[0] = _dot(s, w_ref[0].astype(BF)) + b_ref[0]


def _ada(cond8, ada_w, ada_b):
    tn = 1536
    n6 = 6 * D_MODEL
    return pl.pallas_call(
        _ada_kernel,
        grid=(DEPTH, n6 // tn),
        in_specs=[
            pl.BlockSpec((8, D_MODEL), lambda l, j: (0, 0)),
            pl.BlockSpec((1, D_MODEL, tn), lambda l, j: (l, 0, j)),
            pl.BlockSpec((1, 1, tn), lambda l, j: (l, 0, j)),
        ],
        out_specs=pl.BlockSpec((1, 8, tn), lambda l, j: (l, 0, j)),
        out_shape=jax.ShapeDtypeStruct((DEPTH, 8, n6), F32),
        compiler_params=_params(2),
        name="ada_mod",
    )(cond8, ada_w, ada_b.reshape(DEPTH, 1, n6))


def _inproj_kernel(x_ref, mod_ref, g_ref, w_ref, o_ref):
    m = mod_ref[0]
    h = _rms(x_ref[...]) * g_ref[...] * (1.0 + m[1:2]) + m[0:1]
    o_ref[...] = _dot(h.astype(BF), w_ref[...])


def _inproj(x, mod, g, w, tm=512):
    n, p = x.shape[0], w.shape[1]
    tiles_per_group = (n // mod.shape[0]) // tm
    return pl.pallas_call(
        _inproj_kernel,
        grid=(n // tm,),
        in_specs=[
            pl.BlockSpec((tm, D_MODEL), lambda i: (i, 0)),
            pl.BlockSpec((1, 6, D_MODEL), lambda i: (i // tiles_per_group, 0, 0)),
            _resident((1, D_MODEL)),
            _resident((D_MODEL, p)),
        ],
        out_specs=pl.BlockSpec((tm, p), lambda i: (i, 0)),
        out_shape=jax.ShapeDtypeStruct((n, p), F32),
        compiler_params=_params(1),
        name="in_proj",
    )(x, mod, g.reshape(1, D_MODEL), w)


def _outffn_kernel(ma_ref, mb_ref, x_ref, mod_ref, g2_ref, wo_ref, w1_ref, w3_ref, w2_ref, gf_ref, o_ref, *, final):
    m = mod_ref[0]
    half = ma_ref.shape[1]
    out = _dot(ma_ref[...].astype(BF), wo_ref[0:half, :]) + _dot(mb_ref[...].astype(BF), wo_ref[half:2 * half, :])
    x2 = x_ref[...] + m[2:3] * out
    h2 = (_rms(x2) * g2_ref[...] * (1.0 + m[4:5]) + m[3:4]).astype(BF)
    a = _dot(h2, w1_ref[...])
    b = _dot(h2, w3_ref[...])
    f = _dot((_silu(a) * b).astype(BF), w2_ref[...])
    x3 = x2 + m[5:6] * f
    if final:
        x3 = _rms(x3) * gf_ref[...]
    o_ref[...] = x3


def _outffn(mix_a, mix_b, x, mod, g2, wo, w1, w3, w2, gf, final, tm=256):
    n, half = mix_a.shape
    tiles_per_group = (n // mod.shape[0]) // tm
    row = lambda i: (i, 0)
    return pl.pallas_call(
        functools.partial(_outffn_kernel, final=final),
        grid=(n // tm,),
        in_specs=[
            pl.BlockSpec((tm, half), row),
            pl.BlockSpec((tm, half), row),
            pl.BlockSpec((tm, D_MODEL), row),
            pl.BlockSpec((1, 6, D_MODEL), lambda i: (i // tiles_per_group, 0, 0)),
            _resident((1, D_MODEL)),
            _resident((2 * half, D_MODEL)),
            _resident((D_MODEL, D_FF)),
            _resident((D_MODEL, D_FF)),
            _resident((D_FF, D_MODEL)),
            _resident((1, D_MODEL)),
        ],
        out_specs=pl.BlockSpec((tm, D_MODEL), row),
        out_shape=jax.ShapeDtypeStruct((n, D_MODEL), F32),
        compiler_params=_params(1),
        name="out_ffn",
    )(mix_a, mix_b, x, mod, g2.reshape(1, D_MODEL), wo, w1, w3, w2, gf.reshape(1, D_MODEL))


def _scan_rows(x, rev):
    n = x.shape[0]
    row = lax.broadcasted_iota(jnp.int32, x.shape, 0)
    d = 1
    while d < n:
        if rev:
            x = x + jnp.where(row < n - d, pltpu.roll(x, n - d, 0), 0.0)
        else:
            x = x + jnp.where(row >= d, pltpu.roll(x, d, 0), 0.0)
        d *= 2
    return x


def _gla_masks(rev):
    t = lax.broadcasted_iota(jnp.int32, (CHUNK, CHUNK), 0)
    s = lax.broadcasted_iota(jnp.int32, (CHUNK, CHUNK), 1)
    masks = [t == s]
    g = 1
    while g < CHUNK:
        same = (t // (2 * g)) == (s // (2 * g))
        t_hi, s_hi = (t & g) != 0, (s & g) != 0
        masks.append(same & (~t_hi & s_hi if rev else t_hi & ~s_hi))
        g *= 2
    return masks


def _gla_chunk(q, k, v, lf, st, rev, masks):
    n = CHUNK
    b = _scan_rows(lf, rev)
    row = lax.broadcasted_iota(jnp.int32, b.shape, 0)
    vb = v.astype(BF)
    o = _dot_nt((q * jnp.exp(b)).astype(BF), st.astype(BF))
    a = jnp.where(masks[0], _dot_nt(q.astype(BF), k.astype(BF)), 0.0)
    e = b
    g, lvl = 1, 1
    while g < n:
        bit = (row & g) != 0
        if rev:
            is_q = jnp.logical_not(bit)
            arg = jnp.where(is_q, b - pltpu.roll(e, n - g, 0), e - b)
        else:
            is_q = bit
            arg = jnp.where(is_q, b - pltpu.roll(e, g, 0), e - b)
        m = (jnp.where(is_q, q, k) * jnp.exp(arg)).astype(BF)
        a = a + jnp.where(masks[lvl], _dot_nt(m, m), 0.0)
        if 2 * g < n:
            if rev:
                e = jnp.where(bit, pltpu.roll(e, g, 0), e)
            else:
                e = jnp.where(bit, e, pltpu.roll(e, n - g, 0))
        g *= 2
        lvl += 1
    o = o + _dot(a.astype(BF), vb)
    b_edge = b[0:1, :] if rev else b[n - 1:n, :]
    kd = (k * jnp.exp(b_edge - b)).astype(BF)
    st = st * jnp.exp(b_edge) + _dot_tn(vb, kd)
    return o, st


def _hgrn_kernel(*refs, seq_len, zero_state, write_state):
    qa_ref, ff_ref, fb_ref, ia_ref, ga_ref, lb_ref, g_ref = refs[:7]
    refs = refs[7:]
    if not zero_state:
        s0_ref, refs = refs[0], refs[1:]
    o_ref, refs = refs[0], refs[1:]
    if write_state:
        sf_ref, refs = refs[0], refs[1:]
    ob_scr = refs[0]
    nc = seq_len // CHUNK

    lraw = lb_ref[...]
    lexp = jnp.exp(lraw - jnp.max(lraw, axis=0, keepdims=True))
    lb = lexp[0] / jnp.sum(lexp, axis=0)
    masks_f, masks_b = _gla_masks(False), _gla_masks(True)

    def one_dir(r, f_ref, lbd, st, rev, masks):
        q = _silu(qa_ref[pl.ds(r, CHUNK), :])
        f = lbd + (1.0 - lbd) * _sigmoid(f_ref[pl.ds(r, CHUNK), :])
        return _gla_chunk(q, 1.0 - f, ia_ref[pl.ds(r, CHUNK), :], jnp.log(f), st, rev, masks)

    def body(i, carry):
        st_f, st_b = carry
        rf = pl.multiple_of(i * CHUNK, CHUNK)
        rb = pl.multiple_of((nc - 1 - i) * CHUNK, CHUNK)
        o_f, st_f = one_dir(rf, ff_ref, lb[0:1], st_f, False, masks_f)
        o_ref[pl.ds(rf, CHUNK), :] = o_f
        o_b, st_b = one_dir(rb, fb_ref, lb[1:2], st_b, True, masks_b)
        ob_scr[pl.ds(rb, CHUNK), :] = o_b
        return st_f, st_b

    if zero_state:
        init = (jnp.zeros((DVA, DKA), F32), jnp.zeros((DVA, DKA), F32))
    else:
        init = (s0_ref[0, 0, 0].T, s0_ref[0, 1, 0].T)
    st_f, st_b = lax.fori_loop(0, nc, body, init)
    if write_state:
        sf_ref[0, 0, 0] = st_f.T
        sf_ref[0, 1, 0] = st_b.T

    def epilogue(i, _):
        r = pl.multiple_of(i * CHUNK, CHUNK)
        o = o_ref[pl.ds(r, CHUNK), :] + ob_scr[pl.ds(r, CHUNK), :]
        o_ref[pl.ds(r, CHUNK), :] = _rms(o) * g_ref[...] * _silu(ga_ref[pl.ds(r, CHUNK), :])
        return 0

    lax.fori_loop(0, nc, epilogue, 0)


def _hgrn(p, lb_raw, gain, s0, n_seq, seq_len, write_state):
    zero_state = s0 is None
    col = lambda c: pl.BlockSpec((seq_len, LANE), lambda b, h, c=c: (b, c * HA + h))
    state_spec = pl.BlockSpec((1, 2, 1, DKA, DVA), lambda b, h: (b, 0, h, 0, 0))
    in_specs = [col(0), col(1), col(2), col(3), col(4),
                pl.BlockSpec((lb_raw.shape[0], 2, LANE), lambda b, h: (0, 0, h)),
                pl.BlockSpec((1, DVA), lambda b, h: (0, 0))]
    args = [p, p, p, p, p, lb_raw, gain.reshape(1, DVA)]
    if not zero_state:
        in_specs.append(state_spec)
        args.append(s0)
    out_specs = [pl.BlockSpec((seq_len, LANE), lambda b, h: (b, h))]
    out_shape = [jax.ShapeDtypeStruct((n_seq * seq_len, HA * DVA), F32)]
    if write_state:
        out_specs.append(state_spec)
        out_shape.append(jax.ShapeDtypeStruct((n_seq, 2, HA, DKA, DVA), F32))
    return pl.pallas_call(
        functools.partial(_hgrn_kernel, seq_len=seq_len, zero_state=zero_state, write_state=write_state),
        grid=(n_seq, HA),
        in_specs=in_specs,
        out_specs=out_specs,
        out_shape=out_shape,
        scratch_shapes=[pltpu.VMEM((seq_len, DVA), F32)],
        compiler_params=_params(2),
        name="hgrn2",
    )(*args)


def _log_sigmoid(x):
    return jnp.minimum(x, 0.0) - jnp.log(1.0 + jnp.exp(-jnp.abs(x)))


def _mlstm_chunk(q, k, v, gates, head, cst, nst, mst, rev):
    n = CHUNK
    lane = lax.broadcasted_iota(jnp.int32, (n, LANE), 1)
    sub = lax.broadcasted_iota(jnp.int32, (LANE, n), 0)
    col = lambda x, j: jnp.sum(jnp.where(lane == j, x, 0.0), axis=1, keepdims=True)
    rowv = lambda xt, j: jnp.sum(jnp.where(sub == j, xt, 0.0), axis=0, keepdims=True)
    bsum = _scan_rows(_log_sigmoid(gates), rev)
    ji = head + (2 * HB if rev else 0)
    jf = ji + HB
    b_col, i_col = col(bsum, jf), col(gates, ji)
    b_row, i_row = rowv(bsum.T, jf), rowv(gates.T, ji)
    t = lax.broadcasted_iota(jnp.int32, (n, n), 0)
    s = lax.broadcasted_iota(jnp.int32, (n, n), 1)
    w = jnp.where((s >= t) if rev else (s <= t), b_col - b_row + i_row, -jnp.inf)
    m_t = jnp.maximum(b_col + mst, jnp.max(w, axis=1, keepdims=True))
    qb, kb, vb = q.astype(BF), k.astype(BF), v.astype(BF)
    qk = _dot_nt(qb, kb) * jnp.exp(w - m_t)
    s_state = jnp.exp(b_col + mst - m_t)
    num = _dot(qk.astype(BF), vb) + s_state * _dot(qb, cst.astype(BF))
    den = jnp.sum(qk, axis=1, keepdims=True) + s_state * jnp.sum(q * nst, axis=1, keepdims=True)
    h = num / jnp.maximum(jnp.abs(den), jnp.exp(-m_t))
    b_edge = b_col[0:1, :] if rev else b_col[n - 1:n, :]
    m_new = jnp.maximum(b_edge + mst, jnp.max(b_edge - b_row + i_row, axis=1, keepdims=True))
    a_state = jnp.exp(b_edge + mst - m_new)
    ak = jnp.exp(b_edge - b_col + i_col - m_new) * k
    cst = a_state * cst + _dot_tn(ak.astype(BF), vb)
    nst = a_state * nst + jnp.sum(ak, axis=0, keepdims=True)
    return h, cst, nst, m_new


def _mlstm_kernel(*refs, seq_len, zero_state, write_state):
    q_ref, k_ref, v_ref, og_ref, gt_ref, gb_ref, g_ref = refs[:7]
    refs = refs[7:]
    if not zero_state:
        c0_ref, n0_ref, m0_ref = refs[:3]
        refs = refs[3:]
    o_ref, refs = refs[0], refs[1:]
    if write_state:
        cf_ref, nf_ref, mf_ref = refs[:3]
        refs = refs[3:]
    ob_scr = refs[0]
    nc = seq_len // CHUNK
    head = pl.program_id(1)
    scale = DKB ** -0.5

    def one_dir(r, state, rev):
        q = q_ref[pl.ds(r, CHUNK), :] * scale
        gates = gt_ref[pl.ds(r, CHUNK), :] + gb_ref[...]
        return _mlstm_chunk(q, k_ref[pl.ds(r, CHUNK), :], v_ref[pl.ds(r, CHUNK), :], gates, head, *state, rev)

    def body(i, carry):
        sf, sb = carry
        rf = pl.multiple_of(i * CHUNK, CHUNK)
        rb = pl.multiple_of((nc - 1 - i) * CHUNK, CHUNK)
        h_f, *sf = one_dir(rf, sf, False)
        o_ref[pl.ds(rf, CHUNK), :] = h_f
        h_b, *sb = one_dir(rb, sb, True)
        ob_scr[pl.ds(rb, CHUNK), :] = h_b
        return tuple(sf), tuple(sb)

    if zero_state:
        z = (jnp.zeros((DKB, DVB), F32), jnp.zeros((1, DKB), F32), jnp.zeros((1, 1), F32))
        init = (z, z)
    else:
        init = tuple((c0_ref[0, d, 0], n0_ref[0, d, 0], m0_ref[0, d, 0]) for d in range(2))
    sf, sb = lax.fori_loop(0, nc, body, init)
    if write_state:
        for d, st in enumerate((sf, sb)):
            cf_ref[0, d, 0] = st[0]
            nf_ref[0, d, 0] = st[1]
            mf_ref[0, d, 0] = st[2]

    def epilogue(i, _):
        r = pl.multiple_of(i * CHUNK, CHUNK)
        h = o_ref[pl.ds(r, CHUNK), :] + ob_scr[pl.ds(r, CHUNK), :]
        o_ref[pl.ds(r, CHUNK), :] = _rms(h) * g_ref[...] * _sigmoid(og_ref[pl.ds(r, CHUNK), :])
        return 0

    lax.fori_loop(0, nc, epilogue, 0)


def _mlstm(p, gate_bias, gain, states, n_seq, seq_len, write_state):
    zero_state = states is None
    base = 5 * HA
    col = lambda c: pl.BlockSpec((seq_len, LANE), lambda b, h, c=c: (b, base + c * HB + h))
    c_spec = pl.BlockSpec((1, 2, 1, DKB, DVB), lambda b, h: (b, 0, h, 0, 0))
    n_spec = pl.BlockSpec((1, 2, 1, 1, DKB), lambda b, h: (b, 0, h, 0, 0))
    m_spec = pl.BlockSpec((1, 2, 1, 1, 1), lambda b, h: (b, 0, h, 0, 0))
    in_specs = [col(0), col(1), col(2), col(3),
                pl.BlockSpec((seq_len, LANE), lambda b, h: (b, base + 4 * HB)),
                pl.BlockSpec((1, LANE), lambda b, h: (0, 0)),
                pl.BlockSpec((1, DVB), lambda b, h: (0, 0))]
    bias = jnp.pad(gate_bias.reshape(1, 4 * HB), ((0, 0), (0, LANE - 4 * HB)))
    args = [p, p, p, p, p, bias, gain.reshape(1, DVB)]
    if not zero_state:
        in_specs += [c_spec, n_spec, m_spec]
        args += list(states)
    out_specs = [pl.BlockSpec((seq_len, LANE), lambda b, h: (b, h))]
    out_shape = [jax.ShapeDtypeStruct((n_seq * seq_len, HB * DVB), F32)]
    if write_state:
        out_specs += [c_spec, n_spec, m_spec]
        out_shape += [jax.ShapeDtypeStruct((n_seq, 2, HB, DKB, DVB), F32),
                      jax.ShapeDtypeStruct((n_seq, 2, HB, 1, DKB), F32),
                      jax.ShapeDtypeStruct((n_seq, 2, HB, 1, 1), F32)]
    return pl.pallas_call(
        functools.partial(_mlstm_kernel, seq_len=seq_len, zero_state=zero_state, write_state=write_state),
        grid=(n_seq, HB),
        in_specs=in_specs,
        out_specs=out_specs,
        out_shape=out_shape,
        scratch_shapes=[pltpu.VMEM((seq_len, DVB), F32)],
        compiler_params=_params(2),
        name="mlstm",
    )(*args)


def _rope(x, cos, sin_signed, sw):
    lane = lax.broadcasted_iota(jnp.int32, x.shape, 1)
    first = (lane % (2 * sw)) < sw
    partner = jnp.where(first, pltpu.roll(x, LANE - sw, 1), pltpu.roll(x, sw, 1))
    return x * cos + partner * sin_signed


def _oddprep_kernel(*refs, rope):
    p_ref, qg_ref, kvg_ref, wuq_ref, wuk_ref, wuv_ref = refs[:6]
    refs = refs[6:]
    if rope:
        c64_ref, s64_ref, c32_ref, s32_ref = refs[:4]
        refs = refs[4:]
    qc_ref, kc_ref, vc_ref, qn_ref, qr_ref, ckv_ref, kr_ref, kn_ref, vm_ref = refs
    nq = HQ_C * HD_C
    o_kc, o_vc = nq, nq + HKV_C * HD_C
    o_cq = o_vc + HKV_C * HD_C
    o_ckv = o_cq + D_CQ
    o_kr = o_ckv + D_C
    nn = H_D * NOPE
    qm = _dot((_rms(p_ref[:, o_cq:o_ckv]) * qg_ref[...]).astype(BF), wuq_ref[...])
    qn_ref[...] = qm[:, :nn]
    ckv = _rms(p_ref[:, o_ckv:o_kr]) * kvg_ref[...]
    ckv_ref[...] = ckv
    kn_ref[...] = _dot(ckv.astype(BF), wuk_ref[...])
    vm_ref[...] = _dot(ckv.astype(BF), wuv_ref[...])
    vc_ref[...] = p_ref[:, o_vc:o_cq]
    if rope:
        c64, s64, c32, s32 = c64_ref[...], s64_ref[...], c32_ref[...], s32_ref[...]
        for j in range(nq // LANE):
            qc_ref[:, j * LANE:(j + 1) * LANE] = _rope(p_ref[:, j * LANE:(j + 1) * LANE], c64, s64, HD_C // 4)
        kc_ref[...] = _rope(p_ref[:, o_kc:o_vc], c64, s64, HD_C // 4)
        for j in range(H_D * ROPE // LANE):
            qr_ref[:, j * LANE:(j + 1) * LANE] = _rope(qm[:, nn + j * LANE:nn + (j + 1) * LANE], c32, s32, ROPE // 4)
        kr_ref[...] = _rope(p_ref[:, o_kr:o_kr + LANE], c32, s32, ROPE // 4)
    else:
        qc_ref[...] = p_ref[:, 0:nq]
        kc_ref[...] = p_ref[:, o_kc:o_vc]
        qr_ref[...] = qm[:, nn:]
        kr_ref[...] = p_ref[:, o_kr:o_kr + LANE]


def _oddprep(p, qg, kvg, wuq, wuk, wuv, tables, seq_len, tm=256):
    n = p.shape[0]
    rope = tables is not None
    row = lambda i: (i, 0)
    in_specs = [pl.BlockSpec((tm, P_ODD_PAD), row), _resident((1, D_CQ)), _resident((1, D_C)),
                _resident(wuq.shape), _resident(wuk.shape), _resident(wuv.shape)]
    args = [p, qg.reshape(1, D_CQ), kvg.reshape(1, D_C), wuq, wuk, wuv]
    if rope:
        per_seq = seq_len // tm
        in_specs += [pl.BlockSpec((tm, LANE), lambda i: (i % per_seq, 0))] * 4
        args += list(tables)
    widths = [HQ_C * HD_C, HKV_C * HD_C, HKV_C * HD_C, H_D * NOPE, H_D * ROPE, D_C, LANE, H_D * NOPE, H_D * VD]
    return pl.pallas_call(
        functools.partial(_oddprep_kernel, rope=rope),
        grid=(n // tm,),
        in_specs=in_specs,
        out_specs=[pl.BlockSpec((tm, w), row) for w in widths],
        out_shape=[jax.ShapeDtypeStruct((n, w), F32) for w in widths],
        compiler_params=_params(1),
        name="odd_prep",
    )(*args)


def _ctxkv_kernel(c_ref, wuk_ref, wuv_ref, kn_ref, vm_ref):
    c = c_ref[...].astype(BF)
    kn_ref[...] = _dot(c, wuk_ref[...])
    vm_ref[...] = _dot(c, wuv_ref[...])


def _ctxkv(ckv, wuk, wuv, tm=256):
    n = ckv.shape[0]
    row = lambda i: (i, 0)
    return pl.pallas_call(
        _ctxkv_kernel,
        grid=(n // tm,),
        in_specs=[pl.BlockSpec((tm, D_C), row), _resident(wuk.shape), _resident(wuv.shape)],
        out_specs=[pl.BlockSpec((tm, H_D * NOPE), row), pl.BlockSpec((tm, H_D * VD), row)],
        out_shape=[jax.ShapeDtypeStruct((n, H_D * NOPE), F32), jax.ShapeDtypeStruct((n, H_D * VD), F32)],
        compiler_params=_params(1),
        name="ctx_kv",
    )(ckv, wuk, wuv)


def _swa_kernel(*refs, lat, seq_len, qb):
    if lat:
        q_ref, k_ref, v_ref, kc_ref, vc_ref, sink_ref, o_ref = refs
    else:
        q_ref, k_ref, v_ref, sink_ref, o_ref = refs
    scale = HD_C ** -0.5
    q = q_ref[...].astype(BF)
    if lat:
        band = qb + 2 * WINDOW
        n = pl.program_id(1)
        start = pl.multiple_of(jnp.clip(n * qb - WINDOW, 0, seq_len - band), LANE)
        kb = k_ref[pl.ds(start, band), :].astype(BF)
        vb = v_ref[pl.ds(start, band), :].astype(BF)
        qpos = n * qb + lax.broadcasted_iota(jnp.int32, (qb, band), 0)
        kpos = start + lax.broadcasted_iota(jnp.int32, (qb, band), 1)
        valid = jnp.abs(qpos - kpos) <= WINDOW
        kc = kc_ref[...].astype(BF)
        vc = vc_ref[...].astype(BF)
    else:
        kb = k_ref[...].astype(BF)
        vb = v_ref[...].astype(BF)
    outs = []
    for hq in range(HQ_C):
        hk = hq // G_C
        qh = q[:, hq * HD_C:(hq + 1) * HD_C]
        ks = slice(hk * HD_C, (hk + 1) * HD_C)
        s = _dot_nt(qh, kb[:, ks]) * scale
        if lat:
            s = jnp.where(valid, s, NEG)
        sink = sink_ref[hq:hq + 1, 0:1]
        mx = jnp.maximum(jnp.max(s, axis=1, keepdims=True), sink)
        if lat:
            sc = _dot_nt(qh, kc[:, ks]) * scale
            mx = jnp.maximum(mx, jnp.max(sc, axis=1, keepdims=True))
        p = jnp.exp(s - mx)
        l = jnp.sum(p, axis=1, keepdims=True) + jnp.exp(sink - mx)
        o = _dot(p.astype(BF), vb[:, ks])
        if lat:
            pc = jnp.exp(sc - mx)
            l = l + jnp.sum(pc, axis=1, keepdims=True)
            o = o + _dot(pc.astype(BF), vc[:, ks])
        outs.append(o / l)
    o_ref[...] = jnp.concatenate(outs, axis=1)


def _swa(qc, kc, vc, sink, n_seq, seq_len, ctx_k=None, ctx_v=None, qb=128):
    lat = ctx_k is not None
    nq, nkv = HQ_C * HD_C, HKV_C * HD_C
    sink_b = jnp.broadcast_to(sink.reshape(HQ_C, 1), (HQ_C, LANE))
    if lat:
        past = ctx_k.shape[0] // n_seq
        grid = (n_seq, seq_len // qb)
        in_specs = [pl.BlockSpec((qb, nq), lambda b, n: (b * (seq_len // qb) + n, 0)),
                    pl.BlockSpec((seq_len, nkv), lambda b, n: (b, 0)),
                    pl.BlockSpec((seq_len, nkv), lambda b, n: (b, 0)),
                    pl.BlockSpec((past, nkv), lambda b, n: (b, 0)),
                    pl.BlockSpec((past, nkv), lambda b, n: (b, 0)),
                    pl.BlockSpec((HQ_C, LANE), lambda b, n: (0, 0))]
        args = [qc, kc, vc, ctx_k, ctx_v, sink_b]
        out_spec = pl.BlockSpec((qb, nq), lambda b, n: (b * (seq_len // qb) + n, 0))
    else:
        qb = seq_len
        grid = (n_seq,)
        in_specs = [pl.BlockSpec((seq_len, nq), lambda b: (b, 0)),
                    pl.BlockSpec((seq_len, nkv), lambda b: (b, 0)),
                    pl.BlockSpec((seq_len, nkv), lambda b: (b, 0)),
                    pl.BlockSpec((HQ_C, LANE), lambda b: (0, 0))]
        args = [qc, kc, vc, sink_b]
        out_spec = pl.BlockSpec((seq_len, nq), lambda b: (b, 0))
    return pl.pallas_call(
        functools.partial(_swa_kernel, lat=lat, seq_len=seq_len, qb=qb),
        grid=grid,
        in_specs=in_specs,
        out_specs=out_spec,
        out_shape=jax.ShapeDtypeStruct((n_seq * seq_len, nq), F32),
        compiler_params=_params(len(grid)),
        name="swa_attn",
    )(*args)


def _mla_kernel(*refs, lat):
    if lat:
        qn_ref, qr_ref, kn_ref, kr_ref, v_ref, knc_ref, krc_ref, vc_ref, o_ref = refs
    else:
        qn_ref, qr_ref, kn_ref, kr_ref, v_ref, o_ref = refs
    scale = (NOPE + ROPE) ** -0.5
    qn, qr = qn_ref[...].astype(BF), qr_ref[...].astype(BF)
    kn, v = kn_ref[...].astype(BF), v_ref[...].astype(BF)
    kr = kr_ref[:, 0:ROPE].astype(BF)
    if lat:
        knc, vc, krc = knc_ref[...].astype(BF), vc_ref[...].astype(BF), krc_ref[...].astype(BF)
    outs = []
    for h in range(H_D):
        ns, rs, vs = slice(h * NOPE, (h + 1) * NOPE), slice(h * ROPE, (h + 1) * ROPE), slice(h * VD, (h + 1) * VD)
        s = (_dot_nt(qn[:, ns], kn[:, ns]) + _dot_nt(qr[:, rs], kr)) * scale
        mx = jnp.max(s, axis=1, keepdims=True)
        if lat:
            sc = (_dot_nt(qn[:, ns], knc[:, ns]) + _dot_nt(qr[:, rs], krc)) * scale
            mx = jnp.maximum(mx, jnp.max(sc, axis=1, keepdims=True))
        p = jnp.exp(s - mx)
        l = jnp.sum(p, axis=1, keepdims=True)
        o = _dot(p.astype(BF), v[:, vs])
        if lat:
            pc = jnp.exp(sc - mx)
            l = l + jnp.sum(pc, axis=1, keepdims=True)
            o = o + _dot(pc.astype(BF), vc[:, vs])
        outs.append(o / l)
    o_ref[...] = jnp.concatenate(outs, axis=1)


def _mla(qn, qr, kn, kr, vm, n_seq, seq_len, ctx=None, qb=128):
    lat = ctx is not None
    wn, wr, wv = H_D * NOPE, H_D * ROPE, H_D * VD
    if lat:
        knc, krc, vmc = ctx
        past = knc.shape[0] // n_seq
        per = seq_len // qb
        grid = (n_seq, per)
        qrow = lambda b, n: (b * per + n, 0)
        seq = lambda b, n: (b, 0)
        in_specs = [pl.BlockSpec((qb, wn), qrow), pl.BlockSpec((qb, wr), qrow),
                    pl.BlockSpec((seq_len, wn), seq), pl.BlockSpec((seq_len, LANE), seq),
                    pl.BlockSpec((seq_len, wv), seq),
                    pl.BlockSpec((past, wn), seq), pl.BlockSpec((past, ROPE), seq), pl.BlockSpec((past, wv), seq)]
        args = [qn, qr, kn, kr, vm, knc, krc, vmc]
        out_spec = pl.BlockSpec((qb, wv), qrow)
    else:
        grid = (n_seq,)
        seq = lambda b: (b, 0)
        in_specs = [pl.BlockSpec((seq_len, wn), seq), pl.BlockSpec((seq_len, wr), seq),
                    pl.BlockSpec((seq_len, wn), seq), pl.BlockSpec((seq_len, LANE), seq),
                    pl.BlockSpec((seq_len, wv), seq)]
        args = [qn, qr, kn, kr, vm]
        out_spec = pl.BlockSpec((seq_len, wv), seq)
    return pl.pallas_call(
        functools.partial(_mla_kernel, lat=lat),
        grid=grid,
        in_specs=in_specs,
        out_specs=out_spec,
        out_shape=jax.ShapeDtypeStruct((n_seq * seq_len, wv), F32),
        compiler_params=_params(len(grid)),
        name="mla_attn",
    )(*args)


def _rope_tables(seq_len, d):
    n_rows = seq_len // GRID_W
    rows = jnp.repeat(jnp.arange(n_rows, dtype=F32), GRID_W)
    cols = jnp.tile(jnp.arange(GRID_W, dtype=F32), n_rows)
    d_ax = d // 2
    inv_freq = ROPE_BASE ** (-jnp.arange(0, d_ax, 2, dtype=F32) / d_ax)
    lane = np.arange(LANE) % d
    half, within = lane // d_ax, lane % d_ax
    pair, first = within % (d_ax // 2), within < (d_ax // 2)
    pos = jnp.stack([rows, cols], axis=1)[:, half]
    ang = pos * inv_freq[pair][None, :]
    return jnp.cos(ang), jnp.where(first[None, :], -jnp.sin(ang), jnp.sin(ang))


def _pad_cols(w, width):
    return jnp.pad(w, ((0, 0), (0, width - w.shape[1])))


def kernel(x_prompt, x_sample, state_hgrn, state_mlstm_c, state_mlstm_n, state_mlstm_m, cache_swa_k, cache_swa_v, cache_mla_ckv, cache_mla_krope, c, c_ctx, w_in_even, hgrn_lower_bounds, hgrn_norm_g, mlstm_gate_bias, mlstm_norm_g, w_out_even, w_in_odd, swa_sink, mla_q_norm_g, mla_w_uq, mla_kv_norm_g, mla_w_uk, mla_w_uv, w_out_odd, ada_w, ada_b, norm1_g, norm2_g, ffn_w1, ffn_w3, ffn_w2, final_norm_g):
    assert DEPTH == 2
    batch, seq, _ = x_prompt.shape
    dec_batch, dec_seq, _ = x_sample.shape
    past = cache_swa_k.shape[2]

    cond = jnp.concatenate([c_ctx[None, :], c, jnp.zeros((8 - 1 - dec_batch, D_MODEL), F32)], axis=0)
    mods = _ada(cond, ada_w, ada_b)
    mod_ctx = [mods[l, 0:1].reshape(1, 6, D_MODEL) for l in range(DEPTH)]
    mod_lat = [mods[l, 1:1 + dec_batch].reshape(dec_batch, 6, D_MODEL) for l in range(DEPTH)]

    w_even = _pad_cols(w_in_even[0], P_EVEN_PAD).astype(BF)
    w_odd = _pad_cols(w_in_odd[0], P_ODD_PAD).astype(BF)
    wo_even, wo_odd = w_out_even[0].astype(BF), w_out_odd[0].astype(BF)
    w1, w3, w2 = ffn_w1.astype(BF), ffn_w3.astype(BF), ffn_w2.astype(BF)
    head = np.arange(H_D)[:, None] * (NOPE + ROPE)
    perm = np.concatenate([(head + np.arange(NOPE)[None, :]).ravel(), (head + NOPE + np.arange(ROPE)[None, :]).ravel()])
    wuq = mla_w_uq[0][:, perm].astype(BF)
    wuk, wuv = mla_w_uk[0].astype(BF), mla_w_uv[0].astype(BF)
    tables = _rope_tables(dec_seq, HD_C) + _rope_tables(dec_seq, ROPE)

    def trunk(x, mod, n_seq, seq_len, lat):
        p = _inproj(x, mod[0], norm1_g[0], w_even)
        if lat:
            hg = _hgrn(p, hgrn_lower_bounds, hgrn_norm_g[0], state_hgrn[:, 0], n_seq, seq_len, False)
            ml = _mlstm(p, mlstm_gate_bias[0], mlstm_norm_g[0],
                        (state_mlstm_c[:, 0], state_mlstm_n[:, 0][:, :, :, None, :],
                         state_mlstm_m[:, 0][:, :, :, None, None]), n_seq, seq_len, False)
            oa, hb, new_states = hg[0], ml[0], None
        else:
            oa, s_h = _hgrn(p, hgrn_lower_bounds, hgrn_norm_g[0], None, n_seq, seq_len, True)
            hb, s_c, s_n, s_m = _mlstm(p, mlstm_gate_bias[0], mlstm_norm_g[0], None, n_seq, seq_len, True)
            new_states = (s_h, s_c, s_n, s_m)
        x = _outffn(oa, hb, x, mod[0], norm2_g[0], wo_even, w1[0], w3[0], w2[0], final_norm_g, False)
        p = _inproj(x, mod[1], norm1_g[1], w_odd)
        qc, kc, vc, qn, qr, ckv, kr, kn, vm = _oddprep(p, mla_q_norm_g[0], mla_kv_norm_g[0], wuq, wuk, wuv,
                                                      tables if lat else None, seq_len)
        if lat:
            ctx_k = cache_swa_k[:, 0].reshape(n_seq * past, HKV_C * HD_C)
            ctx_v = cache_swa_v[:, 0].reshape(n_seq * past, HKV_C * HD_C)
            knc, vmc = _ctxkv(cache_mla_ckv[:, 0].reshape(n_seq * past, D_C), wuk, wuv)
            krc = cache_mla_krope[:, 0].reshape(n_seq * past, ROPE)
            oc = _swa(qc, kc, vc, swa_sink[0], n_seq, seq_len, ctx_k, ctx_v)
            od = _mla(qn, qr, kn, kr, vm, n_seq, seq_len, (knc, krc, vmc))
        else:
            oc = _swa(qc, kc, vc, swa_sink[0], n_seq, seq_len)
            od = _mla(qn, qr, kn, kr, vm, n_seq, seq_len)
        y = _outffn(oc, od, x, mod[1], norm2_g[1], wo_odd, w1[1], w3[1], w2[1], final_norm_g, True)
        return y, new_states, (kc, vc, ckv, kr)

    y_ctx, (s_h, s_c, s_n, s_m), (kc, vc, ckv, kr) = trunk(x_prompt.reshape(batch * seq, D_MODEL), mod_ctx, batch, seq, False)
    y_lat, _, _ = trunk(x_sample.reshape(dec_batch * dec_seq, D_MODEL), mod_lat, dec_batch, dec_seq, True)

    return (y_ctx.reshape(batch, seq, D_MODEL),
            y_lat.reshape(dec_batch, dec_seq, D_MODEL),
            s_h.reshape(batch, 1, 2, HA, DKA, DVA),
            s_c.reshape(batch, 1, 2, HB, DKB, DVB),
            s_n.reshape(batch, 1, 2, HB, DKB),
            s_m.reshape(batch, 1, 2, HB),
            kc.reshape(batch, 1, seq, HKV_C, HD_C),
            vc.reshape(batch, 1, seq, HKV_C, HD_C),
            ckv.reshape(batch, 1, seq, D_C),
            kr[:, :ROPE].reshape(batch, 1, seq, ROPE))
```
